```python
import functools
import jax, jax.numpy as jnp
from jax import lax
import numpy as np

D_MODEL = 2048
BATCH = 1
SEQ = 8192
DEPTH = 2
DEC_BATCH = 32
DEC_SEQ = 4
PAST_LEN = 8192
PAGE_SIZE = 128

MIX_WIDTH = D_MODEL
SB_WIDTH = MIX_WIDTH // 2
SB_HEAD_DIM = 128
SB_HEADS = SB_WIDTH // SB_HEAD_DIM
POOL_WIDTH = MIX_WIDTH - SB_WIDTH
POOL_WINDOWS = (2, 4, 8, 16)
POOL_GROUPS = len(POOL_WINDOWS)
POOL_GROUP_DIM = POOL_WIDTH // POOL_GROUPS
POOL_STATE = max(POOL_WINDOWS) - 1
IN_WIDTH = 3 * SB_WIDTH + POOL_WIDTH
D_FF = ((8 * D_MODEL // 3 + 255) // 256) * 256
N_META = 16
Q_BLOCK = 128
RMS_EPS = 1e-6
SB_BIAS_INIT = -5.0

kernel_name = "hymba_stickbreak_pool_macaron_step"


def rmsnorm(x, g):
    xf = x.astype(jnp.float32)
    y = xf * lax.rsqrt(jnp.mean(xf * xf, axis=-1, keepdims=True) + RMS_EPS)
    return (y * g.astype(jnp.float32)).astype(x.dtype)


def swiglu(h, w_gate, w_up, w_down):
    return (jax.nn.silu(h @ w_gate) * (h @ w_up)) @ w_down


def stick_breaking(q, k, v, q_pos, k_pos, sb_bias):
    z = jnp.einsum('bqhd,bshd->bhqs', q, k).astype(jnp.float32) * (SB_HEAD_DIM ** -0.5)
    z = z + sb_bias.astype(jnp.float32)[None, :, None, None]
    causal = k_pos[None, :] < q_pos[:, None]
    log_beta = jax.nn.log_sigmoid(z)
    log_keep = jnp.where(causal, jax.nn.log_sigmoid(-z), 0.0)
    later = lax.cumsum(log_keep, axis=3, reverse=True) - log_keep
    a = jnp.where(causal, jnp.exp(log_beta + later), 0.0)
    return jnp.einsum('bhqs,bshd->bqhd', a.astype(v.dtype), v)


def stick_breaking_prompt(q, k, v, sb_bias):
    b, t, h, d = q.shape
    n_real = t - N_META
    pos = jnp.arange(t)
    o_meta = stick_breaking(q[:, :N_META], k[:, :N_META], v[:, :N_META], pos[:N_META], pos[:N_META], sb_bias)
    n_blocks = n_real // Q_BLOCK
    qb = q[:, N_META:].reshape(b, n_blocks, Q_BLOCK, h, d).transpose(1, 0, 2, 3, 4)
    qpos = (N_META + jnp.arange(n_real)).reshape(n_blocks, Q_BLOCK)
    ob = lax.map(lambda a: stick_breaking(a[0], k, v, a[1], pos, sb_bias), (qb, qpos))
    o_real = ob.transpose(1, 0, 2, 3, 4).reshape(b, n_real, h, d)
    return jnp.concatenate([o_meta, o_real], axis=1)


def pool_mix(u_ext, n_new, pool_w_l, pool_scale_l):
    n_prev = u_ext.shape[1] - n_new
    uf = u_ext.astype(jnp.float32)
    csum = jnp.concatenate([jnp.zeros_like(uf[:, :1]), jnp.cumsum(uf, axis=1)], axis=1)
    hi = n_prev + jnp.arange(n_new) + 1
    u_new = uf[:, n_prev:]
    outs = []
    for g, w in enumerate(POOL_WINDOWS):
        sl = slice(g * POOL_GROUP_DIM, (g + 1) * POOL_GROUP_DIM)
        lo = jnp.maximum(hi - w, 0)
        mean = (csum[:, hi, sl] - csum[:, lo, sl]) / (hi - lo).astype(jnp.float32)[None, :, None]
        diff = (mean - u_new[..., sl]).astype(u_ext.dtype)
        outs.append(jnp.einsum('btc,cd->btd', diff, pool_w_l[g]))
    return jnp.concatenate(outs, axis=-1) * pool_scale_l


def prompt_mixer(q, k, v, u, sb_bias_l, pool_w_l, pool_scale_l):
    o_attn = stick_breaking_prompt(q, k, v, sb_bias_l)
    o_pool = pool_mix(u, u.shape[1], pool_w_l, pool_scale_l)
    return o_attn, o_pool, k, v, u[:, -POOL_STATE:]


def sample_mixer(q, k, v, u, cache_k_l, cache_v_l, state_pool_l, page_table, sb_bias_l, pool_w_l, pool_scale_l):
    b, n_new, h, d = q.shape
    past_k = cache_k_l[page_table].reshape(b, -1, h, d)
    past_v = cache_v_l[page_table].reshape(b, -1, h, d)
    past_len = past_k.shape[1]
    k_all = jnp.concatenate([past_k.astype(k.dtype), k], axis=1)
    v_all = jnp.concatenate([past_v.astype(v.dtype), v], axis=1)
    k_pos = jnp.arange(past_len + n_new)
    q_pos = past_len + jnp.arange(n_new)
    o_attn = stick_breaking(q, k_all, v_all, q_pos, k_pos, sb_bias_l)
    u_ext = jnp.concatenate([state_pool_l.astype(u.dtype), u], axis=1)
    o_pool = pool_mix(u_ext, n_new, pool_w_l, pool_scale_l)
    return o_attn, o_pool, k, v, u_ext[:, -POOL_STATE:]


def trunk_layer(x, mixer, n1, f1g, f1u, f1d, nm, w_in, g_attn, g_pool, w_out, n2, f2g, f2u, f2d):
    b, t, _ = x.shape
    x = x + 0.5 * swiglu(rmsnorm(x, n1), f1g, f1u, f1d)
    proj = rmsnorm(x, nm) @ w_in
    q = proj[..., :SB_WIDTH].reshape(b, t, SB_HEADS, SB_HEAD_DIM)
    k = proj[..., SB_WIDTH:2 * SB_WIDTH].reshape(b, t, SB_HEADS, SB_HEAD_DIM)
    v = proj[..., 2 * SB_WIDTH:3 * SB_WIDTH].reshape(b, t, SB_HEADS, SB_HEAD_DIM)
    u = proj[..., 3 * SB_WIDTH:]
    o_attn, o_pool, k_rows, v_rows, pool_rows = mixer(q, k, v, u)
    merged = jnp.concatenate([rmsnorm(o_attn.reshape(b, t, SB_WIDTH), g_attn),
                              rmsnorm(o_pool, g_pool)], axis=-1)
    x = x + merged @ w_out
    x = x + 0.5 * swiglu(rmsnorm(x, n2), f2g, f2u, f2d)
    return x, k_rows, v_rows, pool_rows


def setup_inputs(seed: int = 0) -> dict:
    key = jax.random.key(seed)
    ks = jax.random.split(key, 24)
    n_pages = PAST_LEN // PAGE_SIZE
    n_used = DEC_BATCH * n_pages
    n_pool = (n_used * 5) // 4
    f32 = jnp.float32

    def nrm(k, shape, scale=1.0):
        return jax.random.normal(k, shape, f32) * scale

    def gain(k, shape):
        return 1.0 + 0.02 * jax.random.normal(k, shape, f32)

    page_table = jax.random.permutation(ks[5], n_pool)[:n_used].reshape(DEC_BATCH, n_pages).astype(jnp.int32)
    return {
        "x_prompt": nrm(ks[0], (BATCH, SEQ, D_MODEL)),
        "x_sample": nrm(ks[1], (DEC_BATCH, DEC_SEQ, D_MODEL)),
        "cache_k": nrm(ks[2], (DEPTH, n_pool, PAGE_SIZE, SB_HEADS, SB_HEAD_DIM)),
        "cache_v": nrm(ks[3], (DEPTH, n_pool, PAGE_SIZE, SB_HEADS, SB_HEAD_DIM)),
        "state_pool": nrm(ks[4], (DEPTH, DEC_BATCH, POOL_STATE, POOL_WIDTH)),
        "page_table": page_table,
        "meta_tokens": nrm(ks[6], (N_META, D_MODEL)),
        "norm_ffn1": gain(ks[7], (DEPTH, D_MODEL)),
        "ffn1_gate": nrm(ks[8], (DEPTH, D_MODEL, D_FF), D_MODEL ** -0.5),
        "ffn1_up": nrm(ks[9], (DEPTH, D_MODEL, D_FF), D_MODEL ** -0.5),
        "ffn1_down": nrm(ks[10], (DEPTH, D_FF, D_MODEL), D_FF ** -0.5),
        "norm_mix": gain(ks[11], (DEPTH, D_MODEL)),
        "w_in": nrm(ks[12], (DEPTH, D_MODEL, IN_WIDTH), D_MODEL ** -0.5),
        "sb_logit_bias": SB_BIAS_INIT + 0.3 * jax.random.normal(ks[23], (DEPTH, SB_HEADS), f32),
        "pool_w": nrm(ks[13], (DEPTH, POOL_GROUPS, POOL_GROUP_DIM, POOL_GROUP_DIM), POOL_GROUP_DIM ** -0.5),
        "pool_scale": gain(ks[14], (DEPTH, POOL_WIDTH)),
        "norm_attn_out": gain(ks[15], (DEPTH, SB_WIDTH)),
        "norm_pool_out": gain(ks[16], (DEPTH, POOL_WIDTH)),
        "w_out": nrm(ks[17], (DEPTH, MIX_WIDTH, D_MODEL), MIX_WIDTH ** -0.5),
        "norm_ffn2": gain(ks[18], (DEPTH, D_MODEL)),
        "ffn2_gate": nrm(ks[19], (DEPTH, D_MODEL, D_FF), D_MODEL ** -0.5),
        "ffn2_up": nrm(ks[20], (DEPTH, D_MODEL, D_FF), D_MODEL ** -0.5),
        "ffn2_down": nrm(ks[21], (DEPTH, D_FF, D_MODEL), D_FF ** -0.5),
        "norm_final": gain(ks[22], (D_MODEL,)),
    }


def reference(x_prompt, x_sample, cache_k, cache_v, state_pool, page_table, meta_tokens,
              norm_ffn1, ffn1_gate, ffn1_up, ffn1_down, norm_mix, w_in, sb_logit_bias, pool_w, pool_scale,
              norm_attn_out, norm_pool_out, w_out, norm_ffn2, ffn2_gate, ffn2_up, ffn2_down,
              norm_final):
    b = x_prompt.shape[0]
    meta = jnp.broadcast_to(meta_tokens.astype(x_prompt.dtype)[None], (b, N_META, D_MODEL))
    xp = jnp.concatenate([meta, x_prompt], axis=1)
    xs = x_sample
    kp_list, vp_list, pp_list, ks_list, vs_list, ps_list = [], [], [], [], [], []
    for l in range(DEPTH):
        lw = (norm_ffn1[l], ffn1_gate[l], ffn1_up[l], ffn1_down[l], norm_mix[l], w_in[l],
              norm_attn_out[l], norm_pool_out[l], w_out[l],
              norm_ffn2[l], ffn2_gate[l], ffn2_up[l], ffn2_down[l])
        p_mix = functools.partial(prompt_mixer, sb_bias_l=sb_logit_bias[l],
                                  pool_w_l=pool_w[l], pool_scale_l=pool_scale[l])
        s_mix = functools.partial(sample_mixer, cache_k_l=cache_k[l], cache_v_l=cache_v[l],
                                  state_pool_l=state_pool[l], page_table=page_table,
                                  sb_bias_l=sb_logit_bias[l],
                                  pool_w_l=pool_w[l], pool_scale_l=pool_scale[l])
        xp, kp, vp, pp = trunk_layer(xp, p_mix, *lw)
        xs, ksm, vsm, psm = trunk_layer(xs, s_mix, *lw)
        kp_list.append(kp); vp_list.append(vp); pp_list.append(pp)
        ks_list.append(ksm); vs_list.append(vsm); ps_list.append(psm)
    y_prompt = rmsnorm(xp, norm_final)[:, N_META:]
    y_sample = rmsnorm(xs, norm_final)
    k_prompt = jnp.stack(kp_list)
    v_prompt = jnp.stack(vp_list)
    pool_prompt = jnp.stack(pp_list)
    k_sample = jnp.stack(ks_list)
    v_sample = jnp.stack(vs_list)
    pool_sample = jnp.stack(ps_list)
    return (y_prompt, y_sample, k_prompt, v_prompt, pool_prompt, k_sample, v_sample, pool_sample)
```

```python
import functools

import jax
import jax.numpy as jnp
from jax import lax
from jax.experimental import pallas as pl
from jax.experimental.pallas import tpu as pltpu

F32 = jnp.float32
BF16 = jnp.bfloat16

RMS_EPS = 1e-6
N_META = 16
HEAD_DIM = 128
POOL_WINDOWS = (2, 4, 8, 16)
POOL_STATE = max(POOL_WINDOWS) - 1
POOL_HALO = 16
ATT_BLOCK = 256
TAIL_ROWS = ATT_BLOCK
NEW_KEY_ROWS = 16
NEW_KEY_TILE = 128
PAGES_PER_STEP = 8
V7X_VMEM_BYTES = 64 * 1024 * 1024
VMEM_LIMIT_BYTES = V7X_VMEM_BYTES - 6 * 1024 * 1024


def _params(*semantics):
    return pltpu.CompilerParams(dimension_semantics=semantics, vmem_limit_bytes=VMEM_LIMIT_BYTES)


def _rms_scale(x):
    return x * lax.rsqrt(jnp.mean(x * x, axis=-1, keepdims=True) + RMS_EPS)


def _softplus(z):
    return jnp.maximum(z, 0.0) + jnp.log(1.0 + jnp.exp(-jnp.abs(z)))


def _ffn_kernel(x_ref, g_ref, wg_ref, wu_ref, wd_ref, o_ref, h_ref, *, n_f):
    j = pl.program_id(1)

    @pl.when(j == 0)
    def _():
        h_ref[...] = (_rms_scale(x_ref[...]) * g_ref[...]).astype(BF16)
        o_ref[...] = jnp.zeros_like(o_ref)

    h = h_ref[...]
    gate = jnp.dot(h, wg_ref[...].astype(BF16), preferred_element_type=F32)
    up = jnp.dot(h, wu_ref[...].astype(BF16), preferred_element_type=F32)
    act = (gate * jax.nn.sigmoid(gate) * up).astype(BF16)
    o_ref[...] += jnp.dot(act, wd_ref[...].astype(BF16), preferred_element_type=F32)

    @pl.when(j == n_f - 1)
    def _():
        o_ref[...] = x_ref[...] + 0.5 * o_ref[...]


def _ffn(x, gain, w_gate, w_up, w_down, layer, *, tm, tf):
    m, d = x.shape
    f = w_gate.shape[2]
    return pl.pallas_call(
        functools.partial(_ffn_kernel, n_f=f // tf),
        grid=(m // tm, f // tf),
        in_specs=[
            pl.BlockSpec((tm, d), lambda i, j: (i, 0), pipeline_mode=pl.Buffered(1)),
            pl.BlockSpec((None, 1, d), lambda i, j: (layer, 0, 0)),
            pl.BlockSpec((None, d, tf), lambda i, j: (layer, 0, j)),
            pl.BlockSpec((None, d, tf), lambda i, j: (layer, 0, j)),
            pl.BlockSpec((None, tf, d), lambda i, j: (layer, j, 0)),
        ],
        out_specs=pl.BlockSpec((tm, d), lambda i, j: (i, 0)),
        out_shape=jax.ShapeDtypeStruct((m, d), F32),
        scratch_shapes=[pltpu.VMEM((tm, d), BF16)],
        compiler_params=_params("parallel", "arbitrary"),
        name="ffn",
    )(x, gain, w_gate, w_up, w_down)


def _inproj_kernel(x_ref, g_ref, w_ref, o_ref, ob_ref, h_ref):
    @pl.when(pl.program_id(1) == 0)
    def _():
        h_ref[...] = (_rms_scale(x_ref[...]) * g_ref[...]).astype(BF16)

    p = jnp.dot(h_ref[...], w_ref[...].astype(BF16), preferred_element_type=F32)
    o_ref[...] = p
    ob_ref[...] = p.astype(BF16)


def _inproj(x, gain, w_in, layer, *, tm, tn):
    m, d = x.shape
    n = w_in.shape[2]
    return pl.pallas_call(
        _inproj_kernel,
        grid=(m // tm, n // tn),
        in_specs=[
            pl.BlockSpec((tm, d), lambda i, j: (i, 0), pipeline_mode=pl.Buffered(1)),
            pl.BlockSpec((None, 1, d), lambda i, j: (layer, 0, 0)),
            pl.BlockSpec((None, d, tn), lambda i, j: (layer, 0, j)),
        ],
        out_specs=[
            pl.BlockSpec((tm, tn), lambda i, j: (i, j)),
            pl.BlockSpec((tm, tn), lambda i, j: (i, j)),
        ],
        out_shape=[jax.ShapeDtypeStruct((m, n), F32), jax.ShapeDtypeStruct((m, n), BF16)],
        scratch_shapes=[pltpu.VMEM((tm, d), BF16)],
        compiler_params=_params("parallel", "arbitrary"),
        name="inproj",
    )(x, gain, w_in)


def _outproj_kernel(oa_ref, op_ref, ga_ref, gp_ref, w_ref, x_ref, o_ref, h_ref, *, wa):
    @pl.when(pl.program_id(1) == 0)
    def _():
        h_ref[:, :wa] = (_rms_scale(oa_ref[...]) * ga_ref[...]).astype(BF16)
        h_ref[:, wa:] = (_rms_scale(op_ref[...]) * gp_ref[...]).astype(BF16)

    o_ref[...] = x_ref[...] + jnp.dot(h_ref[...], w_ref[...].astype(BF16), preferred_element_type=F32)


def _outproj(o_attn, o_pool, g_attn, g_pool, w_out, x, layer, *, tm, tn):
    m, d = x.shape
    wa, wp = o_attn.shape[1], o_pool.shape[1]
    return pl.pallas_call(
        functools.partial(_outproj_kernel, wa=wa),
        grid=(m // tm, d // tn),
        in_specs=[
            pl.BlockSpec((tm, wa), lambda i, j: (i, 0)),
            pl.BlockSpec((tm, wp), lambda i, j: (i, 0)),
            pl.BlockSpec((None, 1, wa), lambda i, j: (layer, 0, 0)),
            pl.BlockSpec((None, 1, wp), lambda i, j: (layer, 0, 0)),
            pl.BlockSpec((None, wa + wp, tn), lambda i, j: (layer, 0, j)),
            pl.BlockSpec((tm, tn), lambda i, j: (i, j)),
        ],
        out_specs=pl.BlockSpec((tm, tn), lambda i, j: (i, j)),
        out_shape=jax.ShapeDtypeStruct((m, d), F32),
        scratch_shapes=[pltpu.VMEM((tm, wa + wp), BF16)],
        compiler_params=_params("parallel", "arbitrary"),
        name="outproj",
    )(o_attn, o_pool, g_attn, g_pool, w_out, x)


def _final_norm_kernel(x_ref, g_ref, yp_ref, yt_ref, *, n_real_blocks):
    y = _rms_scale(x_ref[...]) * g_ref[...]
    i = pl.program_id(0)

    @pl.when(i < n_real_blocks)
    def _():
        yp_ref[...] = y

    @pl.when(i == n_real_blocks)
    def _():
        yt_ref[...] = y


def _final_norm(x, gain, n_real):
    m, d = x.shape
    nb = n_real // TAIL_ROWS
    return pl.pallas_call(
        functools.partial(_final_norm_kernel, n_real_blocks=nb),
        grid=(nb + 1,),
        in_specs=[
            pl.BlockSpec((TAIL_ROWS, d), lambda i: (i, 0)),
            pl.BlockSpec((1, d), lambda i: (0, 0)),
        ],
        out_specs=[
            pl.BlockSpec((TAIL_ROWS, d), lambda i: (jnp.minimum(i, nb - 1), 0)),
            pl.BlockSpec((TAIL_ROWS, d), lambda i: (0, 0)),
        ],
        out_shape=[jax.ShapeDtypeStruct((n_real, d), F32), jax.ShapeDtypeStruct((TAIL_ROWS, d), F32)],
        compiler_params=_params("arbitrary"),
        name="final_norm",
    )(x, gain)


def _sb_tile(z, v_blk, tri, r, acc, mask):
    sp = _softplus(z)
    if mask is not None:
        sp = jnp.where(mask, sp, 0.0)
    hi = sp.astype(BF16)
    lo = (sp - hi.astype(F32)).astype(BF16)
    csum = jnp.dot(hi, tri, preferred_element_type=F32) + jnp.dot(lo, tri, preferred_element_type=F32)
    a = jnp.exp(z - (csum + r))
    if mask is not None:
        a = jnp.where(mask, a, 0.0)
    acc = acc + jnp.dot(a.astype(BF16), v_blk, preferred_element_type=F32)
    return r + csum[:, :1], acc


def _attn_prompt_kernel(bias_ref, q_ref, k_ref, v_ref, tri_ref, o_ref, *, n_real, scale):
    i = pl.program_id(1)
    tq = ATT_BLOCK
    q = q_ref[...]
    tri = tri_ref[...]
    bias = bias_ref[...]
    n_real_blocks = n_real // tq

    def logits(k_blk):
        z = lax.dot_general(q, k_blk, (((1,), (1,)), ((), ())), preferred_element_type=F32)
        return z * scale + bias[:, : k_blk.shape[0]]

    def meta_tile(r, acc, causal):
        k_blk = k_ref[pl.ds(n_real, NEW_KEY_TILE), :]
        v_blk = v_ref[pl.ds(n_real, NEW_KEY_TILE), :]
        row = lax.broadcasted_iota(jnp.int32, (tq, NEW_KEY_TILE), 0)
        col = lax.broadcasted_iota(jnp.int32, (tq, NEW_KEY_TILE), 1)
        mask = col < N_META
        if causal:
            mask = mask & (col < row)
        return _sb_tile(logits(k_blk), v_blk, tri[:NEW_KEY_TILE, :NEW_KEY_TILE], r, acc, mask)

    r0 = jnp.zeros((tq, 1), F32)
    acc0 = jnp.zeros((tq, HEAD_DIM), F32)

    @pl.when(i < n_real_blocks)
    def _():
        row = lax.broadcasted_iota(jnp.int32, (tq, tq), 0)
        col = lax.broadcasted_iota(jnp.int32, (tq, tq), 1)
        start = pl.multiple_of(i * tq, tq)
        r, acc = _sb_tile(logits(k_ref[pl.ds(start, tq), :]), v_ref[pl.ds(start, tq), :], tri, r0, acc0,
                          col < row)

        def body(s, carry):
            kb = pl.multiple_of((i - 1 - s) * tq, tq)
            return _sb_tile(logits(k_ref[pl.ds(kb, tq), :]), v_ref[pl.ds(kb, tq), :], tri, *carry, None)

        r, acc = lax.fori_loop(0, i, body, (r, acc))
        _, acc = meta_tile(r, acc, causal=False)
        o_ref[...] = acc

    @pl.when(i == n_real_blocks)
    def _():
        _, acc = meta_tile(r0, acc0, causal=True)
        o_ref[...] = acc


def _attn_prompt(projb, bias_rows, tri, layer, n_real, n_heads):
    m = projb.shape[0]
    tq = ATT_BLOCK
    return pl.pallas_call(
        functools.partial(_attn_prompt_kernel, n_real=n_real, scale=HEAD_DIM ** -0.5),
        grid=(n_heads, m // tq),
        in_specs=[
            pl.BlockSpec((None, None, 1, tq), lambda h, i: (layer, h, 0, 0)),
            pl.BlockSpec((tq, HEAD_DIM), lambda h, i: (i, h)),
            pl.BlockSpec((m, HEAD_DIM), lambda h, i: (0, n_heads + h)),
            pl.BlockSpec((m, HEAD_DIM), lambda h, i: (0, 2 * n_heads + h)),
            pl.BlockSpec((tq, tq), lambda h, i: (0, 0)),
        ],
        out_specs=pl.BlockSpec((tq, HEAD_DIM), lambda h, i: (i, h)),
        out_shape=jax.ShapeDtypeStruct((m, n_heads * HEAD_DIM), F32),
        compiler_params=_params("parallel", "arbitrary"),
        name="attn_prompt",
    )(bias_rows, projb, projb, projb, tri)


def _attn_sample_kernel(pt_ref, bias_ref, qbd_ref, knew_ref, vnew_ref, tri_ref, *refs, n_heads, n_new, scale):
    del pt_ref
    npg = PAGES_PER_STEP
    k_refs, v_refs = refs[:npg], refs[npg:2 * npg]
    o_ref, acc_ref, r_ref, k2_ref, v2_ref = refs[2 * npg:]
    c = pl.program_id(1)
    rows, width = qbd_ref.shape
    page = k2_ref.shape[0] // npg
    qbd = qbd_ref[...]
    tri = tri_ref[...]
    bias = bias_ref[...]

    def logits(k2):
        z = lax.dot_general(qbd, k2, (((1,), (1,)), ((), ())), preferred_element_type=F32)
        return z * scale

    @pl.when(c == 0)
    def _():
        pad = jnp.zeros((NEW_KEY_TILE - NEW_KEY_ROWS, width), BF16)
        k2 = jnp.concatenate([knew_ref[...], pad], axis=0)
        v2 = jnp.concatenate([vnew_ref[...], pad], axis=0)
        row = lax.broadcasted_iota(jnp.int32, (rows, NEW_KEY_TILE), 0)
        col = lax.broadcasted_iota(jnp.int32, (rows, NEW_KEY_TILE), 1)
        mask = col < (row % n_new)
        z = logits(k2) + bias[:, :NEW_KEY_TILE]
        r, acc = _sb_tile(z, v2, tri[:NEW_KEY_TILE, :NEW_KEY_TILE], jnp.zeros((rows, 1), F32),
                          jnp.zeros((rows, width), F32), mask)
        r_ref[...] = r
        acc_ref[...] = acc

    for p in range(npg):
        for h in range(n_heads):
            dst = (slice(p * page, (p + 1) * page), slice(h * HEAD_DIM, (h + 1) * HEAD_DIM))
            k2_ref[dst] = k_refs[p][pl.ds(h, page, stride=n_heads), :].astype(BF16)
            v2_ref[dst] = v_refs[p][pl.ds(h, page, stride=n_heads), :].astype(BF16)

    z = logits(k2_ref[...])
    r = r_ref[...]
    acc = acc_ref[...]
    tk = ATT_BLOCK
    for sub in reversed(range(npg * page // tk)):
        sl = slice(sub * tk, (sub + 1) * tk)
        r, acc = _sb_tile(z[:, sl] + bias, v2_ref[sl, :], tri, r, acc, None)
    r_ref[...] = r
    acc_ref[...] = acc

    @pl.when(c == pl.num_programs(1) - 1)
    def _():
        o_ref[...] = acc


def _attn_sample(page_table, bias_rows, qbd, k_new, v_new, tri, cache_k, cache_v, layer, *, n_heads, n_new):
    n_dec, n_pages = page_table.shape
    rows, width = qbd.shape[1:]
    depth, n_pool, page_rows, _ = cache_k.shape
    page = page_rows // n_heads
    npg = PAGES_PER_STEP
    n_steps = n_pages // npg

    def page_spec(p):
        def index_map(b, c, pt):
            return (layer, pt[b, (n_steps - 1 - c) * npg + p], 0, 0)
        return pl.BlockSpec((None, None, page_rows, HEAD_DIM), index_map)

    grid_spec = pltpu.PrefetchScalarGridSpec(
        num_scalar_prefetch=1,
        grid=(n_dec, n_steps),
        in_specs=[
            pl.BlockSpec((rows, ATT_BLOCK), lambda b, c, pt: (0, 0)),
            pl.BlockSpec((None, rows, width), lambda b, c, pt: (b, 0, 0)),
            pl.BlockSpec((None, NEW_KEY_ROWS, width), lambda b, c, pt: (b, 0, 0)),
            pl.BlockSpec((None, NEW_KEY_ROWS, width), lambda b, c, pt: (b, 0, 0)),
            pl.BlockSpec((ATT_BLOCK, ATT_BLOCK), lambda b, c, pt: (0, 0)),
        ] + [page_spec(p) for p in range(npg)] * 2,
        out_specs=pl.BlockSpec((None, rows, width), lambda b, c, pt: (b, 0, 0)),
        scratch_shapes=[
            pltpu.VMEM((rows, width), F32),
            pltpu.VMEM((rows, 1), F32),
            pltpu.VMEM((npg * page, width), BF16),
            pltpu.VMEM((npg * page, width), BF16),
        ],
    )
    return pl.pallas_call(
        functools.partial(_attn_sample_kernel, n_heads=n_heads, n_new=n_new, scale=HEAD_DIM ** -0.5),
        grid_spec=grid_spec,
        out_shape=jax.ShapeDtypeStruct((n_dec, rows, width), F32),
        compiler_params=_params("parallel", "arbitrary"),
        name="attn_sample",
    )(page_table, bias_rows, qbd, k_new, v_new, tri, *([cache_k] * npg), *([cache_v] * npg))


def _pool_kernel(u_ref, halo_ref, state_ref, w_ref, scale_ref, o_ref, ext_ref, d_ref, *,
                 n_real_blocks, n_dec, n_new):
    i = pl.program_id(0)
    tb = u_ref.shape[0]
    gw = w_ref.shape[1]
    is_tail = i == n_real_blocks

    ext_ref[:POOL_HALO, :] = jnp.where(is_tail, 0.0, halo_ref[...])
    ext_ref[POOL_HALO:, :] = u_ref[...]

    row = lax.broadcasted_iota(jnp.int32, (tb, 1), 0)
    for g, w in enumerate(POOL_WINDOWS):
        cols = slice(g * gw, (g + 1) * gw)
        total = ext_ref[POOL_HALO:, cols]
        for j in range(1, w):
            total = total + ext_ref[pl.ds(POOL_HALO - j, tb), cols]
        count = jnp.where(is_tail, jnp.minimum(row + 1, w), w).astype(F32)
        d_ref[:, cols] = total / count - ext_ref[POOL_HALO:, cols]

    @pl.when(is_tail)
    def _():
        def ext_row(k):
            if k < POOL_STATE:
                return state_ref[k]
            t = k - POOL_STATE
            return u_ref[N_META + t * n_dec:N_META + (t + 1) * n_dec, :]

        for t in range(n_new):
            cur = ext_row(POOL_STATE + t)
            for g, w in enumerate(POOL_WINDOWS):
                cols = slice(g * gw, (g + 1) * gw)
                total = cur[:, cols]
                for j in range(1, w):
                    total = total + ext_row(POOL_STATE + t - j)[:, cols]
                d_ref[N_META + t * n_dec:N_META + (t + 1) * n_dec, cols] = total / float(w) - cur[:, cols]

    for g in range(len(POOL_WINDOWS)):
        cols = slice(g * gw, (g + 1) * gw)
        y = jnp.dot(d_ref[:, cols].astype(BF16), w_ref[g].astype(BF16), preferred_element_type=F32)
        o_ref[:, cols] = y * scale_ref[:, cols]


def _pool(proj, state_t, pool_w, pool_scale, layer, *, n_real, n_dec, n_new):
    m, n = proj.shape
    n_groups, gw = pool_w.shape[1:3]
    width = n_groups * gw
    u_col = n // width - 1
    tb = TAIL_ROWS
    nb = n_real // tb
    halo_per_block = tb // POOL_HALO

    def halo_map(i):
        return (jnp.where(i == 0, nb * halo_per_block, i * halo_per_block - 1), u_col)

    return pl.pallas_call(
        functools.partial(_pool_kernel, n_real_blocks=nb, n_dec=n_dec, n_new=n_new),
        grid=(nb + 1,),
        in_specs=[
            pl.BlockSpec((tb, width), lambda i: (i, u_col)),
            pl.BlockSpec((POOL_HALO, width), halo_map),
            pl.BlockSpec((None, POOL_STATE, n_dec, width), lambda i: (layer, 0, 0, 0)),
            pl.BlockSpec((None, n_groups, gw, gw), lambda i: (layer, 0, 0, 0)),
            pl.BlockSpec((None, 1, width), lambda i: (layer, 0, 0)),
        ],
        out_specs=pl.BlockSpec((tb, width), lambda i: (i, 0)),
        out_shape=jax.ShapeDtypeStruct((m, width), F32),
        scratch_shapes=[pltpu.VMEM((POOL_HALO + tb, width), F32), pltpu.VMEM((tb, width), F32)],
        compiler_params=_params("arbitrary"),
        name="pool",
    )(proj, proj, state_t, pool_w, pool_scale)


def _dense_row_block(m):
    for blocks in (8, 4, 2, 1):
        if m % blocks == 0 and (m // blocks) % 16 == 0:
            return m // blocks
    raise ValueError(f"no dense row block for {m} rows")


def kernel(x_prompt, x_sample, cache_k, cache_v, state_pool, page_table, meta_tokens, norm_ffn1, ffn1_gate,
           ffn1_up, ffn1_down, norm_mix, w_in, sb_logit_bias, pool_w, pool_scale, norm_attn_out, norm_pool_out,
           w_out, norm_ffn2, ffn2_gate, ffn2_up, ffn2_down, norm_final):
    batch, n_real, d = x_prompt.shape
    n_dec, n_new, _ = x_sample.shape
    depth, n_pool, page, n_heads, head_dim = cache_k.shape
    sb_width = n_heads * head_dim
    n_sample = n_dec * n_new
    assert batch == 1 and head_dim == HEAD_DIM and meta_tokens.shape[0] == N_META
    assert n_real % ATT_BLOCK == 0 and N_META + n_sample <= TAIL_ROWS and n_new <= NEW_KEY_ROWS
    assert page_table.shape[1] % PAGES_PER_STEP == 0 and (PAGES_PER_STEP * page) % ATT_BLOCK == 0
    m = n_real + TAIL_ROWS
    s0 = n_real + N_META
    s1 = s0 + n_sample
    tm = _dense_row_block(m)

    def to_rows(a):
        return a.transpose(1, 0, 2).reshape(n_sample, a.shape[-1])

    def from_rows(a):
        return a.reshape(n_new, n_dec, a.shape[-1]).transpose(1, 0, 2)

    x = jnp.concatenate([x_prompt[0], meta_tokens.astype(F32), to_rows(x_sample),
                         jnp.zeros((m - s1, d), F32)], axis=0)

    idx = jnp.arange(ATT_BLOCK)
    tri = (idx[:, None] >= idx[None, :]).astype(BF16)
    head_mask = (jnp.arange(sb_width)[None, :] // HEAD_DIM == jnp.arange(n_heads)[:, None])
    cache_k2 = cache_k.reshape(depth, n_pool, page * n_heads, head_dim)
    cache_v2 = cache_v.reshape(depth, n_pool, page * n_heads, head_dim)
    state_t = state_pool.transpose(0, 2, 1, 3)

    def row3(a):
        return a[:, None, :]

    k_p, v_p, pool_p, k_s, v_s, pool_s = [], [], [], [], [], []
    for l in range(depth):
        x = _ffn(x, row3(norm_ffn1), ffn1_gate, ffn1_up, ffn1_down, l, tm=tm, tf=256)
        proj, projb = _inproj(x, row3(norm_mix), w_in, l, tm=tm, tn=512)

        bias_heads = jnp.broadcast_to(sb_logit_bias[:, :, None, None], (depth, n_heads, 1, ATT_BLOCK))
        o_attn = _attn_prompt(projb, bias_heads, tri, l, n_real, n_heads)

        k_new = from_rows(proj[s0:s1, sb_width:2 * sb_width])
        v_new = from_rows(proj[s0:s1, 2 * sb_width:3 * sb_width])
        u_new = from_rows(proj[s0:s1, 3 * sb_width:])
        q_new = from_rows(projb[s0:s1, :sb_width])
        qbd = jnp.where(head_mask[None, :, None, :], q_new[:, None, :, :], 0).reshape(n_dec, n_heads * n_new, sb_width)
        pad_new = ((0, 0), (0, NEW_KEY_ROWS - n_new), (0, 0))
        bias_rows = jnp.broadcast_to(jnp.repeat(sb_logit_bias[l], n_new)[:, None], (n_heads * n_new, ATT_BLOCK))
        o_full = _attn_sample(page_table, bias_rows, qbd, jnp.pad(k_new.astype(BF16), pad_new),
                              jnp.pad(v_new.astype(BF16), pad_new), tri, cache_k2, cache_v2, l,
                              n_heads=n_heads, n_new=n_new)
        o_full = o_full.reshape(n_dec, n_heads, n_new, n_heads, head_dim)
        hh = jnp.arange(n_heads)
        o_s = o_full[:, hh, :, hh, :]
        o_s = o_s.transpose(1, 2, 0, 3).reshape(n_dec, n_new, sb_width)
        o_attn = lax.dynamic_update_slice(o_attn, to_rows(o_s), (s0, 0))

        o_pool = _pool(proj, state_t, pool_w, row3(pool_scale), l, n_real=n_real, n_dec=n_dec, n_new=n_new)
        x = _outproj(o_attn, o_pool, row3(norm_attn_out), row3(norm_pool_out), w_out, x, l, tm=tm, tn=512)
        x = _ffn(x, row3(norm_ffn2), ffn2_gate, ffn2_up, ffn2_down, l, tm=tm, tf=256)

        def prompt_rows(cols):
            return jnp.concatenate([proj[n_real:s0, cols], proj[:n_real, cols]], axis=0)

        k_p.append(prompt_rows(slice(sb_width, 2 * sb_width)).reshape(1, n_real + N_META, n_heads, head_dim))
        v_p.append(prompt_rows(slice(2 * sb_width, 3 * sb_width)).reshape(1, n_real + N_META, n_heads, head_dim))
        pool_p.append(proj[n_real - POOL_STATE:n_real, 3 * sb_width:][None])
        k_s.append(k_new.reshape(n_dec, n_new, n_heads, head_dim))
        v_s.append(v_new.reshape(n_dec, n_new, n_heads, head_dim))
        pool_s.append(jnp.concatenate([state_pool[l], u_new], axis=1)[:, -POOL_STATE:])

    y_real, y_tail = _final_norm(x, norm_final[None, :], n_real)
    y_sample = from_rows(y_tail[N_META:N_META + n_sample])
    return (y_real[None], y_sample, jnp.stack(k_p), jnp.stack(v_p), jnp.stack(pool_p),
            jnp.stack(k_s), jnp.stack(v_s), jnp.stack(pool_s))
```
